```python
import math
import jax, jax.numpy as jnp
from jax import lax
import numpy as np


D_MODEL = 1024
BATCH = 8
SEQ = 4096
DEPTH = 2
DEC_BATCH = 32
DEC_SEQ = 1
PAST_LEN = 16384
PAGE_SIZE = 128

HEAD_DIM = 64
GM_HEADS = 4
SB_HEADS = 8
MEM_HEADS = 4
GM_WIDTH = GM_HEADS * HEAD_DIM
SB_WIDTH = SB_HEADS * HEAD_DIM
MEM_WIDTH = MEM_HEADS * HEAD_DIM
MIX_WIDTH = GM_WIDTH + SB_WIDTH + MEM_WIDTH
PROJ_WIDTH = 2 * GM_WIDTH + 3 * SB_WIDTH + MEM_WIDTH
SPLITS = (GM_WIDTH, 2 * GM_WIDTH, 2 * GM_WIDTH + SB_WIDTH,
          2 * GM_WIDTH + 2 * SB_WIDTH, 2 * GM_WIDTH + 3 * SB_WIDTH)
CHUNK = 128
Q_BLOCK = 128
N_MEM = 256
D_FF = 2816
CONV_W = 3
ALPHA = (2.0 * DEPTH) ** 0.25
BETA_INIT = (8.0 * DEPTH) ** -0.25
LN_EPS = 1e-5
SB_BIAS_INIT = 8.0

kernel_name = 'hymba_gmlp_stickbreaking_memory_decoder_step'


def layer_norm(x, g, b):
    xf = x.astype(jnp.float32)
    mu = jnp.mean(xf, axis=-1, keepdims=True)
    var = jnp.mean(jnp.square(xf - mu), axis=-1, keepdims=True)
    return ((xf - mu) * lax.rsqrt(var + LN_EPS) * g + b).astype(x.dtype)


def rms_norm(x, g):
    xf = x.astype(jnp.float32)
    ms = jnp.mean(jnp.square(xf), axis=-1, keepdims=True)
    return (xf * lax.rsqrt(ms + LN_EPS) * g).astype(x.dtype)


def _heads(t, n):
    return t.reshape(t.shape[:-1] + (n, HEAD_DIM))


def _project(x, w_in, gm_ln_g, gm_ln_b):
    h = x @ w_in
    hu, hv, hq, hk, hvs, hqm = jnp.split(h, SPLITS, axis=-1)
    u = jax.nn.gelu(hu)
    v = layer_norm(jax.nn.gelu(hv), gm_ln_g, gm_ln_b)
    return (u, v, _heads(hq, SB_HEADS), _heads(hk, SB_HEADS), _heads(hvs, SB_HEADS),
            _heads(hqm, MEM_HEADS))


def spatial_mix(v, ws, bs):
    t = v.shape[-3]
    w = jnp.tril(ws[:, :t, :t])
    bias = jnp.transpose(bs[:, :t])[:, :, None]
    return jnp.einsum('hts,...shd->...thd', w, v) + bias


def stick_breaking(q, k, v, sb_bias, q_pos, k_pos):
    z = jnp.einsum('bqhd,bkhd->bhqk', q.astype(jnp.float32), k.astype(jnp.float32)) * (HEAD_DIM ** -0.5)
    z = z + sb_bias.astype(jnp.float32)[None, :, None, None]
    valid = k_pos[None, :] < q_pos[:, None]
    log_keep = jnp.where(valid, jax.nn.log_sigmoid(-z), 0.0)
    log_after = lax.cumsum(log_keep, axis=3, reverse=True) - log_keep
    w = jnp.where(valid, jnp.exp(jax.nn.log_sigmoid(z) + log_after), 0.0)
    o = jnp.einsum('bhqk,bkhd->bqhd', w, v.astype(jnp.float32))
    return o.astype(q.dtype)


def sb_prompt(q, k, v, sb_bias):
    b, s, h, d = q.shape
    nb = s // Q_BLOCK
    qb = jnp.moveaxis(q.reshape(b, nb, Q_BLOCK, h, d), 1, 0)
    q_pos = jnp.arange(s, dtype=jnp.int32).reshape(nb, Q_BLOCK)
    k_pos = jnp.arange(s, dtype=jnp.int32)

    def block(args):
        qi, pi = args
        return stick_breaking(qi, k, v, sb_bias, pi, k_pos)

    o = lax.map(block, (qb, q_pos))
    return jnp.moveaxis(o, 0, 1).reshape(b, s, h * d)


def mem_attend(q, mk, mv):
    b, t = q.shape[:2]
    s = jnp.einsum('bthd,bmhd->bhtm', q.astype(jnp.float32), mk.astype(jnp.float32)) * (HEAD_DIM ** -0.5)
    p = jax.nn.softmax(s, axis=-1)
    o = jnp.einsum('bhtm,bmhd->bthd', p, mv.astype(jnp.float32))
    return o.astype(q.dtype).reshape(b, t, MEM_WIDTH)


def _merge(o_gm, o_sb, o_mem, g_norm, w_out):
    o = jnp.concatenate([
        rms_norm(o_gm, g_norm[:GM_WIDTH]),
        rms_norm(o_sb, g_norm[GM_WIDTH:GM_WIDTH + SB_WIDTH]),
        rms_norm(o_mem, g_norm[GM_WIDTH + SB_WIDTH:]),
    ], axis=-1)
    return o @ w_out


def conv_ffn(x, prev_rows, w_up, conv_w, conv_b, w_down):
    t = x.shape[1]
    h = x @ w_up
    hp = jnp.concatenate([prev_rows.astype(h.dtype), h], axis=1)
    c = conv_b + conv_w[0] * hp[:, 0:t] + conv_w[1] * hp[:, 1:1 + t] + conv_w[2] * hp[:, 2:2 + t]
    g, up = jnp.split(c, 2, axis=-1)
    y = (jax.nn.silu(g) * up) @ w_down
    return y, hp[:, -(CONV_W - 1):]


def setup_inputs(seed: int = 0) -> dict:
    key = jax.random.key(seed)
    ks = jax.random.split(key, 27)
    n_pages = PAST_LEN // PAGE_SIZE
    n_phys = (DEC_BATCH * n_pages * 5) // 4

    def nrm(k, shape, scale=1.0):
        return jax.random.normal(k, shape, jnp.float32) * scale

    page_table = jax.random.permutation(ks[8], n_phys)[:DEC_BATCH * n_pages]
    page_table = page_table.reshape(DEC_BATCH, n_pages).astype(jnp.int32)
    return {
        'x_prompt': nrm(ks[0], (BATCH, SEQ, D_MODEL)),
        'x_sample': nrm(ks[1], (DEC_BATCH, DEC_SEQ, D_MODEL)),
        'mem_prompt': nrm(ks[2], (BATCH, N_MEM, D_MODEL)),
        'cache_sb_k': nrm(ks[3], (DEPTH, n_phys, PAGE_SIZE, SB_HEADS, HEAD_DIM)),
        'cache_sb_v': nrm(ks[4], (DEPTH, n_phys, PAGE_SIZE, SB_HEADS, HEAD_DIM)),
        'cache_mem_k': nrm(ks[5], (DEPTH, DEC_BATCH, N_MEM, MEM_HEADS, HEAD_DIM)),
        'cache_mem_v': nrm(ks[6], (DEPTH, DEC_BATCH, N_MEM, MEM_HEADS, HEAD_DIM)),
        'state_conv': nrm(ks[7], (DEPTH, DEC_BATCH, CONV_W - 1, 2 * D_FF)),
        'page_table': page_table,
        'w_in': nrm(ks[9], (DEPTH, D_MODEL, PROJ_WIDTH), D_MODEL ** -0.5),
        'gm_ln_g': 1.0 + nrm(ks[10], (DEPTH, GM_WIDTH), 0.01),
        'gm_ln_b': nrm(ks[11], (DEPTH, GM_WIDTH), 0.01),
        'gm_ws': nrm(ks[12], (DEPTH, GM_HEADS, CHUNK, CHUNK), CHUNK ** -0.5),
        'gm_bs': 1.0 + nrm(ks[13], (DEPTH, GM_HEADS, CHUNK), 0.01),
        'sb_bias': -SB_BIAS_INIT + nrm(ks[25], (DEPTH, SB_HEADS), 0.1),
        'w_mem_kv': nrm(ks[14], (DEPTH, D_MODEL, 2 * MEM_WIDTH), D_MODEL ** -0.5),
        'g_norm': 1.0 + nrm(ks[15], (DEPTH, MIX_WIDTH), 0.01),
        'w_out': nrm(ks[16], (DEPTH, MIX_WIDTH, D_MODEL), BETA_INIT * MIX_WIDTH ** -0.5),
        'ln1_g': 1.0 + nrm(ks[17], (DEPTH, D_MODEL), 0.01),
        'ln1_b': nrm(ks[18], (DEPTH, D_MODEL), 0.01),
        'w_up': nrm(ks[19], (DEPTH, D_MODEL, 2 * D_FF), D_MODEL ** -0.5),
        'conv_w': nrm(ks[20], (DEPTH, CONV_W, 2 * D_FF), CONV_W ** -0.5),
        'conv_b': nrm(ks[21], (DEPTH, 2 * D_FF), 0.01),
        'w_down': nrm(ks[22], (DEPTH, D_FF, D_MODEL), BETA_INIT * D_FF ** -0.5),
        'ln2_g': 1.0 + nrm(ks[23], (DEPTH, D_MODEL), 0.01),
        'ln2_b': nrm(ks[24], (DEPTH, D_MODEL), 0.01),
    }


def reference(x_prompt, x_sample, mem_prompt, cache_sb_k, cache_sb_v, cache_mem_k, cache_mem_v,
              state_conv, page_table, w_in, gm_ln_g, gm_ln_b, gm_ws, gm_bs, sb_bias, w_mem_kv, g_norm,
              w_out, ln1_g, ln1_b, w_up, conv_w, conv_b, w_down, ln2_g, ln2_b):
    b, s = x_prompt.shape[:2]
    db, t = x_sample.shape[:2]
    n_pages = PAST_LEN // PAGE_SIZE
    past = n_pages * PAGE_SIZE
    q_pos_s = past + jnp.arange(t, dtype=jnp.int32)
    k_pos_s = jnp.arange(past + t, dtype=jnp.int32)

    xp, xs = x_prompt, x_sample
    gm_v_p, gm_v_s, sbk_p, sbv_p, sbk_s, sbv_s, mk_p, mv_p, conv_p, conv_s = ([] for _ in range(10))
    for l in range(DEPTH):
        u, v, q, k, vs, qm = _project(xp, w_in[l], gm_ln_g[l], gm_ln_b[l])
        vc = v.reshape(b, s // CHUNK, CHUNK, GM_HEADS, HEAD_DIM)
        o_gm = u * spatial_mix(vc, gm_ws[l], gm_bs[l]).reshape(b, s, GM_WIDTH)
        o_sb = sb_prompt(q, k, vs, sb_bias[l])
        mk, mv = jnp.split(mem_prompt @ w_mem_kv[l], 2, axis=-1)
        mk, mv = _heads(mk, MEM_HEADS), _heads(mv, MEM_HEADS)
        o_mem = mem_attend(qm, mk, mv)
        xp = layer_norm(ALPHA * xp + _merge(o_gm, o_sb, o_mem, g_norm[l], w_out[l]), ln1_g[l], ln1_b[l])
        zero_rows = jnp.zeros((b, CONV_W - 1, 2 * D_FF), xp.dtype)
        y, rows = conv_ffn(xp, zero_rows, w_up[l], conv_w[l], conv_b[l], w_down[l])
        xp = layer_norm(ALPHA * xp + y, ln2_g[l], ln2_b[l])
        gm_v_p.append(v[:, -CHUNK:])
        sbk_p.append(k)
        sbv_p.append(vs)
        mk_p.append(mk)
        mv_p.append(mv)
        conv_p.append(rows)

        u, v, q, k, vs, qm = _project(xs, w_in[l], gm_ln_g[l], gm_ln_b[l])
        o_gm = u * spatial_mix(v.reshape(db, t, GM_HEADS, HEAD_DIM), gm_ws[l], gm_bs[l]).reshape(db, t, GM_WIDTH)
        k_past = cache_sb_k[l][page_table].reshape(db, past, SB_HEADS, HEAD_DIM)
        v_past = cache_sb_v[l][page_table].reshape(db, past, SB_HEADS, HEAD_DIM)
        k_all = jnp.concatenate([k_past.astype(k.dtype), k], axis=1)
        v_all = jnp.concatenate([v_past.astype(vs.dtype), vs], axis=1)
        o_sb = stick_breaking(q, k_all, v_all, sb_bias[l], q_pos_s, k_pos_s).reshape(db, t, SB_WIDTH)
        o_mem = mem_attend(qm, cache_mem_k[l], cache_mem_v[l])
        xs = layer_norm(ALPHA * xs + _merge(o_gm, o_sb, o_mem, g_norm[l], w_out[l]), ln1_g[l], ln1_b[l])
        y, rows = conv_ffn(xs, state_conv[l], w_up[l], conv_w[l], conv_b[l], w_down[l])
        xs = layer_norm(ALPHA * xs + y, ln2_g[l], ln2_b[l])
        gm_v_s.append(v)
        sbk_s.append(k)
        sbv_s.append(vs)
        conv_s.append(rows)

    return (xp, xs, jnp.stack(gm_v_p), jnp.stack(gm_v_s), jnp.stack(sbk_p), jnp.stack(sbv_p),
            jnp.stack(sbk_s), jnp.stack(sbv_s), jnp.stack(mk_p), jnp.stack(mv_p),
            jnp.stack(conv_p), jnp.stack(conv_s))
```

```python
import functools

import jax
import jax.numpy as jnp
from jax import lax
from jax.experimental import pallas as pl
from jax.experimental.pallas import tpu as pltpu

F32 = jnp.float32
BF16 = jnp.bfloat16

HEAD_DIM = 64
GM_HEADS = 4
SB_HEADS = 8
MEM_HEADS = 4
GM_WIDTH = GM_HEADS * HEAD_DIM
SB_WIDTH = SB_HEADS * HEAD_DIM
MEM_WIDTH = MEM_HEADS * HEAD_DIM
CHUNK = 128
CONV_W = 3
LN_EPS = 1e-5
QK_SCALE = HEAD_DIM ** -0.5

LANES = 128
SUBLANES = 8
PROJ_TILE = 512
SB_TILE = 256
FFN_TILE = 256
FFN_COLS = 256
DEC_PAGES = 8
VMEM_LIMIT = 56 * 1024 * 1024


def _cparams(sem):
    return pltpu.CompilerParams(dimension_semantics=sem, vmem_limit_bytes=VMEM_LIMIT)


def _ln(x, g, b):
    mu = jnp.mean(x, axis=-1, keepdims=True)
    xc = x - mu
    var = jnp.mean(xc * xc, axis=-1, keepdims=True)
    return xc * lax.rsqrt(var + LN_EPS) * g + b


def _rms(x, g):
    ms = jnp.mean(x * x, axis=-1, keepdims=True)
    return x * lax.rsqrt(ms + LN_EPS) * g


def _gelu(x):
    c = 0.7978845608028654
    return x * (0.5 * (1.0 + jnp.tanh(c * (x + 0.044715 * (x * x * x)))))


def _silu(x):
    return x / (1.0 + jnp.exp(-x))


def _softplus(z):
    return jnp.maximum(z, 0.0) + jnp.log(1.0 + jnp.exp(-jnp.abs(z)))


def _dot(a, b):
    return jnp.dot(a, b, preferred_element_type=F32)


def _head_of_lane(width):
    return lax.shift_right_logical(lax.broadcasted_iota(jnp.int32, (1, width), 1), HEAD_DIM.bit_length() - 1)


def _const_spec(shape):
    nd = len(shape)
    return pl.BlockSpec(shape, lambda *_: (0,) * nd, pipeline_mode=pl.Buffered(1))


def _memkv_kernel(m_ref, w_ref, mk_ref, mv_ref, mkt_ref, mvb_ref):
    kv = _dot(m_ref[...].astype(BF16), w_ref[...])
    mk = kv[:, :MEM_WIDTH]
    mv = kv[:, MEM_WIDTH:]
    mk_ref[...] = mk
    mv_ref[...] = mv
    mkt_ref[...] = mk.T.astype(BF16)
    mvb_ref[...] = mv.astype(BF16)


def _memkv(mem, w_kv):
    b, n_mem, d = mem.shape
    blk = pl.BlockSpec((None, n_mem, MEM_WIDTH), lambda i: (i, 0, 0))
    return pl.pallas_call(
        _memkv_kernel,
        grid=(b,),
        in_specs=[pl.BlockSpec((None, n_mem, d), lambda i: (i, 0, 0)),
                  _const_spec((d, 2 * MEM_WIDTH))],
        out_specs=[blk, blk, pl.BlockSpec((None, MEM_WIDTH, n_mem), lambda i: (i, 0, 0)), blk],
        out_shape=[jax.ShapeDtypeStruct((b, n_mem, MEM_WIDTH), F32),
                   jax.ShapeDtypeStruct((b, n_mem, MEM_WIDTH), F32),
                   jax.ShapeDtypeStruct((b, MEM_WIDTH, n_mem), BF16),
                   jax.ShapeDtypeStruct((b, n_mem, MEM_WIDTH), BF16)],
        compiler_params=_cparams(("parallel",)),
    )(mem, w_kv)


def _mem_heads(qm, mkt, mvb):
    head = _head_of_lane(MEM_WIDTH)
    out = jnp.zeros(qm.shape, F32)
    for h in range(MEM_HEADS):
        sel = head == h
        s = _dot(jnp.where(sel, qm, 0.0).astype(BF16), mkt)
        p = jnp.exp(s - jnp.max(s, axis=-1, keepdims=True))
        o = _dot(p.astype(BF16), mvb) / jnp.sum(p, axis=-1, keepdims=True)
        out = jnp.where(sel, o, out)
    return out


def _proj_kernel(x_ref, w_ref, lng_ref, lnb_ref, ws_ref, bst_ref, mkt_ref, mvb_ref, gng_ref, gnm_ref,
                 ogm_ref, omem_ref, q_ref, kf_ref, vf_ref, kb_ref, vb_ref, gmv_ref, *, tm):
    xb = x_ref[...].astype(BF16)

    def proj(lo, hi):
        return _dot(xb, w_ref[:, lo:hi])

    c0 = 2 * GM_WIDTH
    huv = proj(0, c0)
    u = _gelu(huv[:, :GM_WIDTH])
    v = _ln(_gelu(huv[:, GM_WIDTH:]), lng_ref[...], lnb_ref[...])

    @pl.when(pl.program_id(1) == pl.num_programs(1) - 1)
    def _():
        gmv_ref[...] = v[tm - CHUNK:, :]

    r = lax.broadcasted_iota(jnp.int32, (GM_HEADS * CHUNK, CHUNK), 0) & (CHUNK - 1)
    s = lax.broadcasted_iota(jnp.int32, (GM_HEADS * CHUNK, CHUNK), 1)
    wst = jnp.where(s <= r, ws_ref[...], 0.0).astype(BF16)
    head = _head_of_lane(GM_WIDTH)
    vb = v.astype(BF16)
    mixed = []
    for c in range(tm // CHUNK):
        y = _dot(wst, vb[c * CHUNK:(c + 1) * CHUNK])
        m = y[(GM_HEADS - 1) * CHUNK:]
        for h in range(GM_HEADS - 2, -1, -1):
            m = jnp.where(head == h, y[h * CHUNK:(h + 1) * CHUNK], m)
        mixed.append(u[c * CHUNK:(c + 1) * CHUNK] * (m + bst_ref[...]))
    o_gm = jnp.concatenate(mixed, axis=0)
    ogm_ref[...] = _rms(o_gm, gng_ref[...]).astype(BF16)

    q_ref[...] = (proj(c0, c0 + SB_WIDTH) * QK_SCALE).astype(BF16)
    hk = proj(c0 + SB_WIDTH, c0 + 2 * SB_WIDTH)
    kf_ref[...] = hk
    kb_ref[...] = hk.astype(BF16)
    hv = proj(c0 + 2 * SB_WIDTH, c0 + 3 * SB_WIDTH)
    vf_ref[...] = hv
    vb_ref[...] = hv.astype(BF16)

    qm = proj(c0 + 3 * SB_WIDTH, c0 + 3 * SB_WIDTH + MEM_WIDTH) * QK_SCALE
    o_mem = _mem_heads(qm, mkt_ref[...], mvb_ref[...])
    omem_ref[...] = _rms(o_mem, gnm_ref[...]).astype(BF16)


def _proj(x, w_in, lng, lnb, ws, bst, mkt, mvb, gng, gnm):
    b, s, d = x.shape
    tm = min(PROJ_TILE, s)
    n_mem = mkt.shape[-1]
    pw = w_in.shape[-1]

    def tile(width):
        return pl.BlockSpec((None, tm, width), lambda i, j: (i, j, 0))

    def per_b(r, c):
        return pl.BlockSpec((None, r, c), lambda i, j: (i, 0, 0))

    def sds(width, dt):
        return jax.ShapeDtypeStruct((b, s, width), dt)

    return pl.pallas_call(
        functools.partial(_proj_kernel, tm=tm),
        grid=(b, s // tm),
        in_specs=[tile(d), _const_spec((d, pw)), _const_spec((1, GM_WIDTH)), _const_spec((1, GM_WIDTH)),
                  _const_spec((GM_HEADS * CHUNK, CHUNK)), _const_spec((CHUNK, GM_WIDTH)),
                  per_b(MEM_WIDTH, n_mem), per_b(n_mem, MEM_WIDTH),
                  _const_spec((1, GM_WIDTH)), _const_spec((1, MEM_WIDTH))],
        out_specs=[tile(GM_WIDTH), tile(MEM_WIDTH), tile(SB_WIDTH), tile(SB_WIDTH), tile(SB_WIDTH),
                   tile(SB_WIDTH), tile(SB_WIDTH), per_b(CHUNK, GM_WIDTH)],
        out_shape=[sds(GM_WIDTH, BF16), sds(MEM_WIDTH, BF16), sds(SB_WIDTH, BF16), sds(SB_WIDTH, F32),
                   sds(SB_WIDTH, F32), sds(SB_WIDTH, BF16), sds(SB_WIDTH, BF16),
                   jax.ShapeDtypeStruct((b, CHUNK, GM_WIDTH), F32)],
        compiler_params=_cparams(("parallel", "arbitrary")),
    )(x, w_in, lng, lnb, ws, bst, mkt, mvb, gng, gnm)


def _sb_kernel(bias_ref, q_ref, k_ref, v_ref, u_ref, g_ref, o_ref, acc_ref, carry_ref, *, tq):
    qi = pl.program_id(1)
    low = lax.broadcasted_iota(jnp.int32, (1, LANES), 1) < HEAD_DIM
    row = lax.broadcasted_iota(jnp.int32, (tq, tq), 0)
    col = lax.broadcasted_iota(jnp.int32, (tq, tq), 1)
    valid = col < row
    contract_last = (((1,), (1,)), ((), ()))

    for p in range(SB_HEADS // 2):
        lanes = slice(p * LANES, (p + 1) * LANES)
        qp = q_ref[:, lanes]
        zero = jnp.zeros_like(qp)
        q_pair = (jnp.where(low, qp, zero), jnp.where(low, zero, qp))
        carry_ref[...] = jnp.zeros(carry_ref.shape, F32)
        acc_ref[:, lanes] = jnp.zeros((tq, LANES), F32)

        def step(j, masked):
            off = pl.multiple_of(j * tq, tq)
            kp = k_ref[pl.ds(off, tq), lanes]
            vp = v_ref[pl.ds(off, tq), lanes]
            pv = []
            for e in range(2):
                z = lax.dot_general(q_pair[e], kp, contract_last, preferred_element_type=F32)
                z = z + bias_ref[2 * p + e]
                sp = _softplus(z)
                drop = jnp.where(valid, sp, 0.0) if masked else sp
                after = _dot(drop.astype(BF16), u_ref[...])
                c = carry_ref[e]
                t = (z - sp) - after - jnp.concatenate([c] * (tq // LANES), axis=1)
                w = jnp.exp(t)
                if masked:
                    w = jnp.where(valid, w, 0.0)
                pv.append(_dot(w.astype(BF16), vp))
                carry_ref[e] = c + jnp.sum(drop, axis=1, keepdims=True)
            acc_ref[:, lanes] += jnp.where(low, pv[0], pv[1])

        step(qi, True)

        def body(t, carry):
            step(qi - 1 - t, False)
            return carry

        lax.fori_loop(0, qi, body, 0)

    o_ref[...] = _rms(acc_ref[...], g_ref[...]).astype(BF16)


def _sb_prompt(q, k, v, bias, g):
    b, s, w = q.shape
    tq = min(SB_TILE, s)
    r = lax.broadcasted_iota(jnp.int32, (tq, tq), 0)
    c = lax.broadcasted_iota(jnp.int32, (tq, tq), 1)
    later = (r > c).astype(BF16)
    tile = pl.BlockSpec((None, tq, w), lambda i, j: (i, j, 0))
    full = pl.BlockSpec((None, s, w), lambda i, j: (i, 0, 0))
    return pl.pallas_call(
        functools.partial(_sb_kernel, tq=tq),
        grid=(b, s // tq),
        in_specs=[pl.BlockSpec(memory_space=pltpu.SMEM), tile, full, full,
                  _const_spec((tq, tq)), _const_spec((1, w))],
        out_specs=tile,
        out_shape=jax.ShapeDtypeStruct((b, s, w), BF16),
        scratch_shapes=[pltpu.VMEM((tq, w), F32), pltpu.VMEM((2, tq, LANES), F32)],
        compiler_params=_cparams(("parallel", "arbitrary")),
    )(bias, q, k, v, later, g)


def _shift_rows(h, prev, n):
    top = jnp.where(lax.broadcasted_iota(jnp.int32, (SUBLANES, 1), 0) < n, prev,
                    pltpu.roll(h[:SUBLANES], n, 0))
    if h.shape[0] == SUBLANES:
        return top
    return jnp.concatenate([top, pltpu.roll(h, n, 0)[SUBLANES:]], axis=0)


def _conv_ffn(x1b, wup_ref, cw_ref, cb_ref, wdn_ref, prev2, prev1, d_ff, cols):
    y = None
    tails = {}
    for c0 in range(0, d_ff, cols):
        halves = []
        for base in (c0, d_ff + c0):
            cs = slice(base, base + cols)
            h = _dot(x1b, wup_ref[:, cs])
            tails[base] = h[h.shape[0] - SUBLANES:]
            halves.append(cb_ref[:, cs] + cw_ref[0:1, cs] * _shift_rows(h, prev2(cs), 2)
                          + cw_ref[1:2, cs] * _shift_rows(h, prev1(cs), 1) + cw_ref[2:3, cs] * h)
        act = (_silu(halves[0]) * halves[1]).astype(BF16)
        part = _dot(act, wdn_ref[c0:c0 + cols, :])
        y = part if y is None else y + part
    return y, tails


def _merge_ffn_kernel(x_ref, ogm_ref, osb_ref, omem_ref, wout_ref, l1g_ref, l1b_ref, wup_ref, cw_ref, cb_ref,
                      wdn_ref, l2g_ref, l2b_ref, y_ref, rows_ref, c2_ref, c1_ref, *, alpha, d_ff, cols):
    @pl.when(pl.program_id(1) == 0)
    def _():
        c2_ref[...] = jnp.zeros(c2_ref.shape, F32)
        c1_ref[...] = jnp.zeros(c1_ref.shape, F32)

    o = (_dot(ogm_ref[...], wout_ref[0:GM_WIDTH, :])
         + _dot(osb_ref[...], wout_ref[GM_WIDTH:GM_WIDTH + SB_WIDTH, :])
         + _dot(omem_ref[...], wout_ref[GM_WIDTH + SB_WIDTH:, :]))
    x1 = _ln(alpha * x_ref[...] + o, l1g_ref[...], l1b_ref[...])
    y, tails = _conv_ffn(x1.astype(BF16), wup_ref, cw_ref, cb_ref, wdn_ref,
                         lambda cs: c2_ref[:, cs], lambda cs: c1_ref[:, cs], d_ff, cols)
    for base, tail in tails.items():
        cs = slice(base, base + cols)
        c2_ref[:, cs] = pltpu.roll(tail, 2, 0)
        c1_ref[:, cs] = pltpu.roll(tail, 1, 0)
    y_ref[...] = _ln(alpha * x1 + y, l2g_ref[...], l2b_ref[...])

    @pl.when(pl.program_id(1) == pl.num_programs(1) - 1)
    def _():
        rows_ref[...] = c2_ref[0:CONV_W - 1, :]


def _merge_ffn(x, ogm, osb, omem, w_out, l1g, l1b, w_up, conv_w, conv_b, w_down, l2g, l2b, alpha):
    b, s, d = x.shape
    tm = min(FFN_TILE, s)
    d_ff = w_down.shape[0]
    mix = w_out.shape[0]

    def tile(width):
        return pl.BlockSpec((None, tm, width), lambda i, j: (i, j, 0))

    return pl.pallas_call(
        functools.partial(_merge_ffn_kernel, alpha=alpha, d_ff=d_ff, cols=FFN_COLS),
        grid=(b, s // tm),
        in_specs=[tile(d), tile(GM_WIDTH), tile(SB_WIDTH), tile(MEM_WIDTH), _const_spec((mix, d)),
                  _const_spec((1, d)), _const_spec((1, d)), _const_spec((d, 2 * d_ff)),
                  _const_spec((CONV_W, 2 * d_ff)), _const_spec((1, 2 * d_ff)), _const_spec((d_ff, d)),
                  _const_spec((1, d)), _const_spec((1, d))],
        out_specs=[tile(d), pl.BlockSpec((None, CONV_W - 1, 2 * d_ff), lambda i, j: (i, 0, 0))],
        out_shape=[jax.ShapeDtypeStruct((b, s, d), F32),
                   jax.ShapeDtypeStruct((b, CONV_W - 1, 2 * d_ff), F32)],
        scratch_shapes=[pltpu.VMEM((SUBLANES, 2 * d_ff), F32), pltpu.VMEM((SUBLANES, 2 * d_ff), F32)],
        compiler_params=_cparams(("parallel", "arbitrary")),
    )(x, ogm, osb, omem, w_out, l1g, l1b, w_up, conv_w, conv_b, w_down, l2g, l2b)


def _sample_proj_kernel(x_ref, w_ref, lng_ref, lnb_ref, ws0_ref, bs0_ref,
                        ogm_ref, gmv_ref, q_ref, k_ref, v_ref, qm_ref):
    h = _dot(x_ref[...].astype(BF16), w_ref[...])
    c0 = 2 * GM_WIDTH
    u = _gelu(h[:, :GM_WIDTH])
    v = _ln(_gelu(h[:, GM_WIDTH:c0]), lng_ref[...], lnb_ref[...])
    gmv_ref[...] = v
    ogm_ref[...] = u * (ws0_ref[...] * v + bs0_ref[...])
    q_ref[...] = h[:, c0:c0 + SB_WIDTH] * QK_SCALE
    k_ref[...] = h[:, c0 + SB_WIDTH:c0 + 2 * SB_WIDTH]
    v_ref[...] = h[:, c0 + 2 * SB_WIDTH:c0 + 3 * SB_WIDTH]
    qm_ref[...] = h[:, c0 + 3 * SB_WIDTH:] * QK_SCALE


def _sample_proj(x, w_in, lng, lnb, ws0, bs0):
    n = x.shape[0]
    widths = (GM_WIDTH, GM_WIDTH, SB_WIDTH, SB_WIDTH, SB_WIDTH, MEM_WIDTH)
    return pl.pallas_call(
        _sample_proj_kernel,
        out_shape=[jax.ShapeDtypeStruct((n, w), F32) for w in widths],
        compiler_params=pltpu.CompilerParams(vmem_limit_bytes=VMEM_LIMIT),
    )(x, w_in, lng, lnb, ws0, bs0)


def _decode_sb_kernel(pt_ref, q_ref, bias_ref, hsel_ref, later_ref, hexp_ref, *refs, n_pages, page):
    del pt_ref
    k_refs = refs[:DEC_PAGES]
    v_refs = refs[DEC_PAGES:2 * DEC_PAGES]
    o_ref, acc_ref, carry_ref = refs[2 * DEC_PAGES:]
    g = pl.program_id(1)

    @pl.when(g == 0)
    def _():
        acc_ref[...] = jnp.zeros(acc_ref.shape, F32)
        carry_ref[...] = jnp.zeros(carry_ref.shape, F32)

    q = q_ref[...]
    for i in range(DEC_PAGES):
        kq = (k_refs[i][...] * q).astype(BF16)
        z = _dot(kq, hsel_ref[...]) + bias_ref[...]
        sp = _softplus(z)
        after = _dot(later_ref[...], sp.astype(BF16))
        carry = carry_ref[...]
        w = jnp.exp((z - sp) - after - carry)
        wexp = _dot(w.astype(BF16), hexp_ref[...])
        pv = wexp * v_refs[i][...]
        acc_ref[...] += jnp.sum(pv.reshape(page // SUBLANES, SUBLANES, SB_WIDTH), axis=0)
        carry_ref[...] = carry + jnp.sum(sp, axis=0, keepdims=True)

    @pl.when(g == pl.num_programs(1) - 1)
    def _():
        o_ref[...] = jnp.sum(acc_ref[...], axis=0, keepdims=True)


def _decode_sb(q, bias_lanes, cache_k, cache_v, page_table, layer):
    db = q.shape[0]
    n_pages = page_table.shape[1]
    page = cache_k.shape[2]
    assert n_pages % DEC_PAGES == 0
    steps = n_pages // DEC_PAGES
    hsel = (lax.broadcasted_iota(jnp.int32, (SB_WIDTH, LANES), 0) // HEAD_DIM
            == lax.broadcasted_iota(jnp.int32, (SB_WIDTH, LANES), 1)).astype(BF16)
    later = (lax.broadcasted_iota(jnp.int32, (page, page), 1)
             > lax.broadcasted_iota(jnp.int32, (page, page), 0)).astype(BF16)

    def page_spec(i):
        def index(r, g, pt):
            return (layer, pt[r * n_pages + (n_pages - 1 - (g * DEC_PAGES + i))], 0, 0)
        return pl.BlockSpec((None, None, page, SB_WIDTH), index)

    def const(shape):
        return pl.BlockSpec(shape, lambda r, g, pt: (0,) * len(shape))

    row = pl.BlockSpec((None, 1, SB_WIDTH), lambda r, g, pt: (r, 0, 0))
    grid_spec = pltpu.PrefetchScalarGridSpec(
        num_scalar_prefetch=1,
        grid=(db, steps),
        in_specs=[row, const((1, LANES)), const((SB_WIDTH, LANES)), const((page, page)),
                  const((LANES, SB_WIDTH))]
                 + [page_spec(i) for i in range(DEC_PAGES)] * 2,
        out_specs=row,
        scratch_shapes=[pltpu.VMEM((SUBLANES, SB_WIDTH), F32), pltpu.VMEM((1, LANES), F32)],
    )
    return pl.pallas_call(
        functools.partial(_decode_sb_kernel, n_pages=n_pages, page=page),
        grid_spec=grid_spec,
        out_shape=jax.ShapeDtypeStruct((db, 1, SB_WIDTH), F32),
        compiler_params=_cparams(("parallel", "arbitrary")),
    )(page_table.reshape(-1), q, bias_lanes, hsel, later, hsel.T,
      *([cache_k] * DEC_PAGES), *([cache_v] * DEC_PAGES))


def _decode_mem_kernel(qm_ref, mk_ref, mv_ref, hsel_ref, hexp_ref, o_ref):
    n_mem = mk_ref.shape[0]
    kq = (mk_ref[...] * qm_ref[...]).astype(BF16)
    s = _dot(kq, hsel_ref[...])
    p = jnp.exp(s - jnp.max(s, axis=0, keepdims=True))
    p = p / jnp.sum(p, axis=0, keepdims=True)
    pv = _dot(p.astype(BF16), hexp_ref[...]) * mv_ref[...]
    part = jnp.sum(pv.reshape(n_mem // SUBLANES, SUBLANES, MEM_WIDTH), axis=0)
    o_ref[...] = jnp.sum(part, axis=0, keepdims=True)


def _decode_mem(qm, mk, mv):
    db, n_mem, _ = mk.shape
    hsel = (lax.broadcasted_iota(jnp.int32, (MEM_WIDTH, LANES), 0) // HEAD_DIM
            == lax.broadcasted_iota(jnp.int32, (MEM_WIDTH, LANES), 1)).astype(BF16)
    row = pl.BlockSpec((None, 1, MEM_WIDTH), lambda r: (r, 0, 0))
    blk = pl.BlockSpec((None, n_mem, MEM_WIDTH), lambda r: (r, 0, 0))
    return pl.pallas_call(
        _decode_mem_kernel,
        grid=(db,),
        in_specs=[row, blk, blk, pl.BlockSpec((MEM_WIDTH, LANES), lambda r: (0, 0)),
                  pl.BlockSpec((LANES, MEM_WIDTH), lambda r: (0, 0))],
        out_specs=row,
        out_shape=jax.ShapeDtypeStruct((db, 1, MEM_WIDTH), F32),
        compiler_params=_cparams(("parallel",)),
    )(qm, mk, mv, hsel, hsel.T)


def _sample_ffn_kernel(x_ref, ogm_ref, osb_ref, omem_ref, gn_ref, wout_ref, l1g_ref, l1b_ref, wup_ref,
                       st0_ref, st1_ref, cw_ref, cb_ref, wdn_ref, l2g_ref, l2b_ref, y_ref, h_ref,
                       *, alpha, d_ff):
    g_gm = gn_ref[:, 0:GM_WIDTH]
    g_sb = gn_ref[:, GM_WIDTH:GM_WIDTH + SB_WIDTH]
    g_mem = gn_ref[:, GM_WIDTH + SB_WIDTH:]
    o = (_dot(_rms(ogm_ref[...], g_gm).astype(BF16), wout_ref[0:GM_WIDTH, :])
         + _dot(_rms(osb_ref[...], g_sb).astype(BF16), wout_ref[GM_WIDTH:GM_WIDTH + SB_WIDTH, :])
         + _dot(_rms(omem_ref[...], g_mem).astype(BF16), wout_ref[GM_WIDTH + SB_WIDTH:, :]))
    x1 = _ln(alpha * x_ref[...] + o, l1g_ref[...], l1b_ref[...])
    h = _dot(x1.astype(BF16), wup_ref[...])
    h_ref[...] = h
    c = cb_ref[...] + cw_ref[0:1, :] * st0_ref[...] + cw_ref[1:2, :] * st1_ref[...] + cw_ref[2:3, :] * h
    act = (_silu(c[:, :d_ff]) * c[:, d_ff:]).astype(BF16)
    y_ref[...] = _ln(alpha * x1 + _dot(act, wdn_ref[...]), l2g_ref[...], l2b_ref[...])


def _sample_ffn(x, ogm, osb, omem, gn, w_out, l1g, l1b, w_up, st0, st1, conv_w, conv_b, w_down, l2g, l2b,
                alpha):
    n, d = x.shape
    d_ff = w_down.shape[0]
    return pl.pallas_call(
        functools.partial(_sample_ffn_kernel, alpha=alpha, d_ff=d_ff),
        out_shape=[jax.ShapeDtypeStruct((n, d), F32), jax.ShapeDtypeStruct((n, 2 * d_ff), F32)],
        compiler_params=pltpu.CompilerParams(vmem_limit_bytes=VMEM_LIMIT),
    )(x, ogm, osb, omem, gn, w_out, l1g, l1b, w_up, st0, st1, conv_w, conv_b, w_down, l2g, l2b)


def kernel(x_prompt, x_sample, mem_prompt, cache_sb_k, cache_sb_v, cache_mem_k, cache_mem_v, state_conv, page_table, w_in, gm_ln_g, gm_ln_b, gm_ws, gm_bs, sb_bias, w_mem_kv, g_norm, w_out, ln1_g, ln1_b, w_up, conv_w, conv_b, w_down, ln2_g, ln2_b):
    depth = w_in.shape[0]
    b, s, d = x_prompt.shape
    db, t = x_sample.shape[:2]
    assert t == 1 and s % CHUNK == 0
    alpha = (2.0 * depth) ** 0.25
    n_phys, page = cache_sb_k.shape[1:3]
    n_mem = cache_mem_k.shape[2]
    d_ff = w_down.shape[1]

    w_in_b, w_kv_b, w_out_b = w_in.astype(BF16), w_mem_kv.astype(BF16), w_out.astype(BF16)
    w_up_b, w_down_b = w_up.astype(BF16), w_down.astype(BF16)
    cache_k = cache_sb_k.reshape(depth, n_phys, page, SB_WIDTH)
    cache_v = cache_sb_v.reshape(depth, n_phys, page, SB_WIDTH)
    cmem_k = cache_mem_k.reshape(depth, db, n_mem, MEM_WIDTH)
    cmem_v = cache_mem_v.reshape(depth, db, n_mem, MEM_WIDTH)

    def row(a):
        return a.reshape(1, -1)

    xp = x_prompt
    xs = x_sample.reshape(db, d)
    outs = [[] for _ in range(10)]
    for l in range(depth):
        gn = row(g_norm[l])
        gn_gm, gn_sb, gn_mem = gn[:, :GM_WIDTH], gn[:, GM_WIDTH:GM_WIDTH + SB_WIDTH], gn[:, GM_WIDTH + SB_WIDTH:]
        lng, lnb = row(gm_ln_g[l]), row(gm_ln_b[l])
        ffn_w = (w_out_b[l], row(ln1_g[l]), row(ln1_b[l]), w_up_b[l])
        ffn_w2 = (conv_w[l], row(conv_b[l]), w_down_b[l], row(ln2_g[l]), row(ln2_b[l]))

        mk, mv, mkt, mvb = _memkv(mem_prompt, w_kv_b[l])
        ws_rows = gm_ws[l].reshape(GM_HEADS * CHUNK, CHUNK)
        bs_lanes = jnp.repeat(gm_bs[l].T, HEAD_DIM, axis=1)
        ogm, omem, q, kf, vf, kb, vb, gmv = _proj(xp, w_in_b[l], lng, lnb, ws_rows, bs_lanes, mkt, mvb,
                                                   gn_gm, gn_mem)
        osb = _sb_prompt(q, kb, vb, sb_bias[l], gn_sb)
        xp, rows = _merge_ffn(xp, ogm, osb, omem, *ffn_w, *ffn_w2, alpha)

        ws0 = row(jnp.repeat(gm_ws[l][:, 0, 0], HEAD_DIM))
        bs0 = row(jnp.repeat(gm_bs[l][:, 0], HEAD_DIM))
        ogm_s, gmv_s, q_s, k_s, v_s, qm_s = _sample_proj(xs, w_in_b[l], lng, lnb, ws0, bs0)
        bias_lanes = jnp.zeros((1, LANES), F32).at[0, :SB_HEADS].set(sb_bias[l])
        osb_s = _decode_sb(q_s.reshape(db, 1, SB_WIDTH), bias_lanes, cache_k, cache_v, page_table, l)
        omem_s = _decode_mem(qm_s.reshape(db, 1, MEM_WIDTH), cmem_k[l], cmem_v[l])
        xs, h_s = _sample_ffn(xs, ogm_s, osb_s.reshape(db, SB_WIDTH), omem_s.reshape(db, MEM_WIDTH), gn,
                              *ffn_w, state_conv[l][:, 0], state_conv[l][:, 1], *ffn_w2, alpha)
        rows_s = jnp.stack([state_conv[l][:, 1], h_s], axis=1)

        new = (gmv, gmv_s.reshape(db, 1, GM_WIDTH),
               kf.reshape(b, s, SB_HEADS, HEAD_DIM), vf.reshape(b, s, SB_HEADS, HEAD_DIM),
               k_s.reshape(db, 1, SB_HEADS, HEAD_DIM), v_s.reshape(db, 1, SB_HEADS, HEAD_DIM),
               mk.reshape(b, n_mem, MEM_HEADS, HEAD_DIM), mv.reshape(b, n_mem, MEM_HEADS, HEAD_DIM),
               rows, rows_s)
        for acc, a in zip(outs, new):
            acc.append(a)

    return (xp, xs.reshape(db, 1, d)) + tuple(jnp.stack(a) for a in outs)
```
